```python
import math
import jax, jax.numpy as jnp
from jax import lax
import numpy as np

D_MODEL = 1024
BATCH = 4
SEQ = 4096
DEPTH = 2

D_RG = D_MODEL
RG_HEADS = 8
RG_HEAD_DIM = D_RG // RG_HEADS
D_ML = D_MODEL
ML_HEADS = 4
ML_HEAD_DIM = D_ML // ML_HEADS
D_MIX = D_RG + D_ML
D_IN = 2 * D_RG + 3 * D_ML
CONV_WIDTH = 4
RG_C = 8.0
ML_CHUNK = 128
EPS = 1e-6

kernel_name = "hymba_style_rglru_mlstm_hybrid"


def rms_norm(x, g):
    x32 = x.astype(jnp.float32)
    y = x32 * lax.rsqrt(jnp.mean(x32 * x32, axis=-1, keepdims=True) + EPS)
    return (y * g.astype(jnp.float32)).astype(x.dtype)


def causal_depthwise_conv(x, w, b):
    ch = x.shape[-1]
    y = lax.conv_general_dilated(
        x, w[:, None, :].astype(x.dtype), window_strides=(1,),
        padding=[(CONV_WIDTH - 1, 0)], dimension_numbers=("NWC", "WIO", "NWC"),
        feature_group_count=ch)
    return y + b.astype(x.dtype)


def block_diag(x, w):
    h, dh, dout = w.shape
    xb = x.reshape(x.shape[:-1] + (h, dh))
    return jnp.einsum("bshd,hde->bshe", xb, w).reshape(x.shape[:-1] + (h * dout,))


def rglru(x, w_a, b_a, w_x, b_x, lam):
    r = jax.nn.sigmoid(block_diag(x, w_a) + b_a).astype(jnp.float32)
    i = jax.nn.sigmoid(block_diag(x, w_x) + b_x).astype(jnp.float32)
    log_a = -RG_C * r * jax.nn.softplus(-lam.astype(jnp.float32))
    a = jnp.exp(log_a)
    u = jnp.sqrt(-jnp.expm1(2.0 * log_a)) * (i * x.astype(jnp.float32))

    def combine(lhs, rhs):
        a1, b1 = lhs
        a2, b2 = rhs
        return a1 * a2, a2 * b1 + b2

    _, h = lax.associative_scan(combine, (a, u), axis=1)
    return h.astype(x.dtype)


def mlstm_chunkwise(q, k, v, log_i, log_f):
    bsz, s_len, nh, dh = q.shape
    nc = s_len // ML_CHUNK

    def to_chunks(t):
        return t.reshape(bsz, nc, ML_CHUNK, nh, dh).transpose(1, 0, 3, 2, 4)

    def gate_chunks(t):
        return t.reshape(bsz, nc, ML_CHUNK, nh).transpose(1, 0, 3, 2)

    causal = jnp.tril(jnp.ones((ML_CHUNK, ML_CHUNK), dtype=bool))

    def step(carry, xs):
        c_st, n_st, m_st = carry
        qc, kc, vc, li, lf = xs
        b = jnp.cumsum(lf, axis=-1)
        b_last = b[..., -1]
        d = jnp.where(causal, b[..., :, None] - b[..., None, :] + li[..., None, :], -jnp.inf)
        m_inter = b + m_st[..., None]
        m_t = jnp.maximum(m_inter, jnp.max(d, axis=-1))
        w_intra = jnp.exp(d - m_t[..., None])
        w_inter = jnp.exp(m_inter - m_t)
        s = jnp.einsum("bhtk,bhsk->bhts", qc, kc) * w_intra
        num = (jnp.einsum("bhts,bhsv->bhtv", s, vc)
               + w_inter[..., None] * jnp.einsum("bhtk,bhkv->bhtv", qc, c_st))
        den = jnp.sum(s, axis=-1) + w_inter * jnp.einsum("bhtk,bhk->bht", qc, n_st)
        h = num / jnp.maximum(jnp.abs(den), jnp.exp(-m_t))[..., None]
        g = b_last[..., None] - b + li
        m_new = jnp.maximum(b_last + m_st, jnp.max(g, axis=-1))
        w_state = jnp.exp(g - m_new[..., None])
        decay = jnp.exp(b_last + m_st - m_new)
        kw = kc * w_state[..., None]
        c_new = decay[..., None, None] * c_st + jnp.einsum("bhsk,bhsv->bhkv", kw, vc)
        n_new = decay[..., None] * n_st + jnp.sum(kw, axis=2)
        return (c_new, n_new, m_new), h

    init = (jnp.zeros((bsz, nh, dh, dh), jnp.float32),
            jnp.zeros((bsz, nh, dh), jnp.float32),
            jnp.zeros((bsz, nh), jnp.float32))
    _, h = lax.scan(step, init, (to_chunks(q), to_chunks(k), to_chunks(v),
                                 gate_chunks(log_i), gate_chunks(log_f)))
    return h.transpose(1, 0, 3, 2, 4).reshape(bsz, s_len, nh, dh)


def mlstm_branch(xm, o_pre, conv_w, conv_b, w_q, w_k, w_v, w_if, b_if, head_g):
    bsz, s_len, _ = xm.shape
    xc = jax.nn.silu(causal_depthwise_conv(xm, conv_w, conv_b))
    q = block_diag(xc, w_q)
    k = block_diag(xc, w_k)
    v = block_diag(xm, w_v)
    gates = (jnp.concatenate([q, k, v], axis=-1) @ w_if + b_if).astype(jnp.float32)
    log_i = gates[..., :ML_HEADS]
    log_f = jax.nn.log_sigmoid(gates[..., ML_HEADS:])

    def heads(t):
        return t.reshape(bsz, s_len, ML_HEADS, ML_HEAD_DIM).astype(jnp.float32)

    cell = mlstm_chunkwise(heads(q), heads(k) * (ML_HEAD_DIM ** -0.5), heads(v), log_i, log_f)
    h = jax.nn.sigmoid(heads(o_pre)) * cell
    h = h * lax.rsqrt(jnp.mean(h * h, axis=-1, keepdims=True) + EPS)
    return (h.reshape(bsz, s_len, D_ML) * head_g.astype(jnp.float32)).astype(xm.dtype)


def setup_inputs(seed: int = 0) -> dict:
    key = jax.random.key(seed)
    ks = jax.random.split(key, 24)
    f32 = jnp.float32

    def nrm(k, shape, scale):
        return jax.random.normal(k, shape, f32) * scale

    x = jax.random.normal(ks[0], (BATCH, SEQ, D_MODEL), f32)
    c = jax.random.normal(ks[1], (BATCH, D_MODEL), f32)
    norm_g = 1.0 + nrm(ks[2], (DEPTH, D_MODEL), 0.02)
    w_ada = nrm(ks[3], (DEPTH, D_MODEL, 3 * D_MODEL), 0.3 * D_MODEL ** -0.5)
    b_ada = nrm(ks[4], (DEPTH, 3 * D_MODEL), 0.02)
    w_in = nrm(ks[5], (DEPTH, D_MODEL, D_IN), D_MODEL ** -0.5)
    rg_conv_w = nrm(ks[6], (DEPTH, CONV_WIDTH, D_RG), CONV_WIDTH ** -0.5)
    rg_conv_b = nrm(ks[7], (DEPTH, D_RG), 0.02)
    rg_w_a = nrm(ks[8], (DEPTH, RG_HEADS, RG_HEAD_DIM, RG_HEAD_DIM), RG_HEAD_DIM ** -0.5)
    rg_b_a = nrm(ks[9], (DEPTH, D_RG), 0.02)
    rg_w_x = nrm(ks[10], (DEPTH, RG_HEADS, RG_HEAD_DIM, RG_HEAD_DIM), RG_HEAD_DIM ** -0.5)
    rg_b_x = nrm(ks[11], (DEPTH, D_RG), 0.02)
    a_c = jax.random.uniform(ks[12], (DEPTH, D_RG), f32, 0.9, 0.999)
    a0 = a_c ** (1.0 / RG_C)
    rg_lambda = jnp.log(a0) - jnp.log1p(-a0)
    ml_conv_w = nrm(ks[13], (DEPTH, CONV_WIDTH, D_ML), CONV_WIDTH ** -0.5)
    ml_conv_b = nrm(ks[14], (DEPTH, D_ML), 0.02)
    ml_w_q = nrm(ks[15], (DEPTH, ML_HEADS, ML_HEAD_DIM, ML_HEAD_DIM), ML_HEAD_DIM ** -0.5)
    ml_w_k = nrm(ks[16], (DEPTH, ML_HEADS, ML_HEAD_DIM, ML_HEAD_DIM), ML_HEAD_DIM ** -0.5)
    ml_w_v = nrm(ks[17], (DEPTH, ML_HEADS, ML_HEAD_DIM, ML_HEAD_DIM), ML_HEAD_DIM ** -0.5)
    ml_w_if = nrm(ks[18], (DEPTH, 3 * D_ML, 2 * ML_HEADS), 0.1 * (3 * D_ML) ** -0.5)
    b_i = nrm(ks[19], (DEPTH, ML_HEADS), 0.1) - 1.0
    b_f = jnp.linspace(3.0, 6.0, ML_HEADS, dtype=f32)[None, :] + nrm(ks[20], (DEPTH, ML_HEADS), 0.1)
    ml_b_if = jnp.concatenate([b_i, b_f], axis=-1)
    ml_norm_g = 1.0 + nrm(ks[21], (DEPTH, D_ML), 0.02)
    w_out = nrm(ks[22], (DEPTH, D_MIX, D_MODEL), D_MIX ** -0.5)
    final_g = 1.0 + nrm(ks[23], (D_MODEL,), 0.02)
    return {"x": x, "c": c, "norm_g": norm_g, "w_ada": w_ada, "b_ada": b_ada,
            "w_in": w_in, "rg_conv_w": rg_conv_w, "rg_conv_b": rg_conv_b,
            "rg_w_a": rg_w_a, "rg_b_a": rg_b_a, "rg_w_x": rg_w_x, "rg_b_x": rg_b_x,
            "rg_lambda": rg_lambda, "ml_conv_w": ml_conv_w, "ml_conv_b": ml_conv_b,
            "ml_w_q": ml_w_q, "ml_w_k": ml_w_k, "ml_w_v": ml_w_v, "ml_w_if": ml_w_if,
            "ml_b_if": ml_b_if, "ml_norm_g": ml_norm_g, "w_out": w_out, "final_g": final_g}


def reference(x, c, norm_g, w_ada, b_ada, w_in, rg_conv_w, rg_conv_b, rg_w_a, rg_b_a,
              rg_w_x, rg_b_x, rg_lambda, ml_conv_w, ml_conv_b, ml_w_q, ml_w_k, ml_w_v,
              ml_w_if, ml_b_if, ml_norm_g, w_out, final_g):
    split_pts = [D_RG, 2 * D_RG, 2 * D_RG + D_ML, 2 * D_RG + 2 * D_ML]
    c_act = jax.nn.silu(c)
    for l in range(DEPTH):
        mod = c_act @ w_ada[l] + b_ada[l]
        shift, scale, gate = jnp.split(mod, 3, axis=-1)
        h = rms_norm(x, norm_g[l]) * (1.0 + scale[:, None, :]) + shift[:, None, :]
        u = h @ w_in[l]
        rg_x, rg_z, ml_x, ml_o, ml_z = jnp.split(u, split_pts, axis=-1)
        y_rg = rglru(causal_depthwise_conv(rg_x, rg_conv_w[l], rg_conv_b[l]),
                     rg_w_a[l], rg_b_a[l], rg_w_x[l], rg_b_x[l], rg_lambda[l]) * jax.nn.silu(rg_z)
        y_ml = mlstm_branch(ml_x, ml_o, ml_conv_w[l], ml_conv_b[l], ml_w_q[l], ml_w_k[l],
                            ml_w_v[l], ml_w_if[l], ml_b_if[l], ml_norm_g[l]) * jax.nn.silu(ml_z)
        y = jnp.concatenate([y_rg, y_ml], axis=-1) @ w_out[l]
        x = x + gate[:, None, :] * y
    return rms_norm(x, final_g)
```

```python
import functools

import jax
import jax.numpy as jnp
from jax import lax
from jax.experimental import pallas as pl
from jax.experimental.pallas import tpu as pltpu

F32 = jnp.float32
BF16 = jnp.bfloat16

EPS = 1e-6
RG_C = 8.0
CONV_WIDTH = 4
RG_HEADS = 8
ML_HEADS = 4
ML_CHUNK = 128
N_PROJ = 5
SUBLANES = 8
LANES = 128
SEQ_TILE = 256
VMEM_LIMIT_BYTES = 56 * 1024 * 1024


def _dot(a, b):
    return jnp.dot(a, b, preferred_element_type=F32)


def _dot_nt(a, b):
    return lax.dot_general(a, b, (((1,), (1,)), ((), ())), preferred_element_type=F32)


def _dot_tn(a, b):
    return lax.dot_general(a, b, (((0,), (0,)), ((), ())), preferred_element_type=F32)


def _softplus(z):
    return jnp.maximum(z, 0.0) + jnp.log1p(jnp.exp(-jnp.abs(z)))


def _silu(z):
    return z * jax.nn.sigmoid(z)


def _lane_scan(x, op, fill):
    axis = x.ndim - 1
    lane = lax.broadcasted_iota(jnp.int32, x.shape, axis)
    d = 1
    while d < x.shape[axis]:
        x = op(x, jnp.where(lane >= d, pltpu.roll(x, d, axis), fill))
        d *= 2
    return x


def _block_diag(xb, w_ref, heads):
    dh = xb.shape[1] // heads
    return jnp.concatenate(
        [_dot(xb[:, h * dh:(h + 1) * dh], w_ref[h]) for h in range(heads)], axis=1)


def _causal_conv(u, ext_ref, w_ref, b_ref):
    t = u.shape[0]
    ext_ref[SUBLANES:SUBLANES + t, :] = u
    w = w_ref[...]
    y = w[CONV_WIDTH - 1:CONV_WIDTH] * u + b_ref[...]
    for k in range(1, CONV_WIDTH):
        y = y + w[CONV_WIDTH - 1 - k:CONV_WIDTH - k] * ext_ref[SUBLANES - k:SUBLANES - k + t, :]
    ext_ref[0:SUBLANES, :] = u[t - SUBLANES:t, :]
    return y


def _rg_scan(a, u, hcarry_ref, h_ref):
    t = a.shape[0]
    row = lax.broadcasted_iota(jnp.int32, a.shape, 0)
    d = 1
    while d < SUBLANES:
        a_sh = jnp.where(row >= d, pltpu.roll(a, d, 0), 1.0)
        u_sh = jnp.where(row >= d, pltpu.roll(u, d, 0), 0.0)
        u = a * u_sh + u
        a = a * a_sh
        d *= 2
    h = a[0:SUBLANES] * hcarry_ref[...] + u[0:SUBLANES]
    h_ref[0:SUBLANES, :] = h
    for j in range(1, t // SUBLANES):
        rows = slice(j * SUBLANES, (j + 1) * SUBLANES)
        h = a[rows] * h + u[rows]
        h_ref[rows, :] = h
    hcarry_ref[...] = h[SUBLANES - 1:SUBLANES, :]


def _mlstm_chunk(q, k, v, qb, kb, vb, li, lf, c_ref, n_ref, m_ref):
    L = q.shape[0]
    dh = q.shape[1] // ML_HEADS
    b = _lane_scan(lf, jnp.add, 0.0)
    g = li - b
    m_prev = m_ref[...][:, 0:1]
    big_m = jnp.maximum(m_prev, _lane_scan(g, jnp.maximum, -jnp.inf))
    m_last = big_m[:, L - 1:L]
    w_inter = jnp.exp(m_prev - big_m)
    inv_stab = jnp.exp(-(b + big_m))
    w_state = jnp.exp(g - m_last)
    decay = w_inter[:, L - 1:L]
    m_ref[...] = jnp.broadcast_to(b[:, L - 1:L] + m_last, m_ref.shape)

    rows = jnp.concatenate(
        [big_m, w_inter, inv_stab, w_state, jnp.zeros((L - 4 * SUBLANES, L), F32)], axis=0)
    cols = jnp.transpose(rows)
    tri = (lax.broadcasted_iota(jnp.int32, (L, L), 0)
           >= lax.broadcasted_iota(jnp.int32, (L, L), 1))
    out = []
    for h in range(ML_HEADS):
        hs = slice(h * dh, (h + 1) * dh)
        m_col = cols[:, h:h + 1]
        w_inter_col = cols[:, SUBLANES + h:SUBLANES + h + 1]
        inv_stab_col = cols[:, 2 * SUBLANES + h:2 * SUBLANES + h + 1]
        w_state_col = cols[:, 3 * SUBLANES + h:3 * SUBLANES + h + 1]
        w_intra = jnp.exp(jnp.where(tri, g[h:h + 1, :] - m_col, -jnp.inf))
        s = _dot_nt(qb[:, hs], kb[:, hs]) * w_intra
        c_old = c_ref[h]
        n_old = n_ref[h]
        num = _dot(s.astype(BF16), vb[:, hs]) + w_inter_col * _dot(qb[:, hs], c_old.astype(BF16))
        den = (jnp.sum(s, axis=1, keepdims=True)
               + w_inter_col * jnp.sum(q[:, hs] * n_old, axis=1, keepdims=True))
        out.append(num / jnp.maximum(jnp.abs(den), inv_stab_col))
        kw = k[:, hs] * w_state_col
        dec = decay[h:h + 1, :]
        c_ref[h] = dec * c_old + _dot_tn(kw.astype(BF16), vb[:, hs])
        n_ref[h] = dec * n_old + jnp.sum(kw, axis=0, keepdims=True)
    return jnp.concatenate(out, axis=1)


def _rms_norm(x, g):
    return x * lax.rsqrt(jnp.mean(x * x, axis=-1, keepdims=True) + EPS) * g


def _layer_kernel(x_ref, mod_ref, ng_ref, w_in_ref,
                  rcw_ref, rcb_ref, rwa_ref, rba_ref, rwx_ref, rbx_ref, lam_ref,
                  mcw_ref, mcb_ref, wq_ref, wk_ref, wv_ref, wif_ref, bif_ref, mng_ref,
                  w_out_ref, fg_ref,
                  o_ref,
                  rgx_ext, mlx_ext, hcarry, h_scr, c_st, n_st, m_st, *, is_last):
    d_model = x_ref.shape[2]
    t = x_ref.shape[1]

    @pl.when(pl.program_id(1) == 0)
    def _():
        rgx_ext[0:SUBLANES, :] = jnp.zeros((SUBLANES, d_model), F32)
        mlx_ext[0:SUBLANES, :] = jnp.zeros((SUBLANES, d_model), F32)
        hcarry[...] = jnp.zeros_like(hcarry)
        c_st[...] = jnp.zeros_like(c_st)
        n_st[...] = jnp.zeros_like(n_st)
        m_st[...] = jnp.zeros_like(m_st)

    x = x_ref[0]
    shift = mod_ref[0, 0:1, :]
    scale = mod_ref[0, 1:2, :]
    gate = mod_ref[0, 2:3, :]
    hb = (_rms_norm(x, ng_ref[...]) * (1.0 + scale) + shift).astype(BF16)

    def proj(i):
        return _dot(hb, w_in_ref[:, i * d_model:(i + 1) * d_model])

    xc = _causal_conv(proj(0), rgx_ext, rcw_ref, rcb_ref)
    xcb = xc.astype(BF16)
    r = jax.nn.sigmoid(_block_diag(xcb, rwa_ref, RG_HEADS) + rba_ref[...])
    i_gate = jax.nn.sigmoid(_block_diag(xcb, rwx_ref, RG_HEADS) + rbx_ref[...])
    log_a = (-RG_C) * r * _softplus(-lam_ref[...])
    a = jnp.exp(log_a)
    mult = jnp.sqrt(-jnp.tanh(log_a) * (a * a + 1.0))
    _rg_scan(a, mult * (i_gate * xc), hcarry, h_scr)
    y_rg = (h_scr[...] * _silu(proj(1))).astype(BF16)

    ml_x = proj(2)
    xc2b = _silu(_causal_conv(ml_x, mlx_ext, mcw_ref, mcb_ref)).astype(BF16)
    ml_xb = ml_x.astype(BF16)
    q = _block_diag(xc2b, wq_ref, ML_HEADS)
    k = _block_diag(xc2b, wk_ref, ML_HEADS)
    v = _block_diag(ml_xb, wv_ref, ML_HEADS)
    qb, kb, vb = q.astype(BF16), k.astype(BF16), v.astype(BF16)
    gates = (_dot_nt(wif_ref[:, 0:d_model], qb)
             + _dot_nt(wif_ref[:, d_model:2 * d_model], kb)
             + _dot_nt(wif_ref[:, 2 * d_model:3 * d_model], vb)) + bif_ref[...]
    li = gates[0:SUBLANES, :]
    lf = -_softplus(-gates[SUBLANES:2 * SUBLANES, :])
    cells = []
    for c in range(t // ML_CHUNK):
        rs = slice(c * ML_CHUNK, (c + 1) * ML_CHUNK)
        cells.append(_mlstm_chunk(q[rs], k[rs], v[rs], qb[rs], kb[rs], vb[rs],
                                  li[:, rs], lf[:, rs], c_st, n_st, m_st))
    hm = jax.nn.sigmoid(proj(3)) * jnp.concatenate(cells, axis=0)
    dh = d_model // ML_HEADS
    hm = jnp.concatenate(
        [hm[:, h * dh:(h + 1) * dh]
         * lax.rsqrt(jnp.mean(hm[:, h * dh:(h + 1) * dh] ** 2, axis=-1, keepdims=True) + EPS)
         for h in range(ML_HEADS)], axis=1)
    y_ml = (hm * mng_ref[...] * _silu(proj(4))).astype(BF16)

    y = _dot(y_rg, w_out_ref[0:d_model, :]) + _dot(y_ml, w_out_ref[d_model:2 * d_model, :])
    x_new = x + gate * y
    if is_last:
        x_new = _rms_norm(x_new, fg_ref[...])
    o_ref[0] = x_new


def _mod_kernel(c_ref, w_ref, b_ref, o_ref):
    c = c_ref[...]
    o_ref[0] = _dot(_silu(c).astype(BF16), w_ref[0].astype(BF16)) + b_ref[0]


def _modulation(c, w_ada, b_ada):
    depth, d_model, _ = w_ada.shape
    batch = c.shape[0]
    rows = -(-batch // SUBLANES) * SUBLANES
    c_pad = jnp.pad(c, ((0, rows - batch), (0, 0)))
    out = pl.pallas_call(
        _mod_kernel,
        grid=(depth, 3),
        in_specs=[
            pl.BlockSpec((rows, d_model), lambda l, n: (0, 0)),
            pl.BlockSpec((1, d_model, d_model), lambda l, n: (l, 0, n)),
            pl.BlockSpec((1, 1, d_model), lambda l, n: (l, 0, n)),
        ],
        out_specs=pl.BlockSpec((1, rows, d_model), lambda l, n: (l, 0, n)),
        out_shape=jax.ShapeDtypeStruct((depth, rows, 3 * d_model), F32),
        compiler_params=pltpu.CompilerParams(
            dimension_semantics=("arbitrary", "arbitrary"),
            vmem_limit_bytes=VMEM_LIMIT_BYTES),
        name="adaln_modulation",
    )(c_pad, w_ada, b_ada.reshape(depth, 1, 3 * d_model))
    return out[:, :batch, :].reshape(depth, batch, 3, d_model)


def _layer(x, mod_l, layer, is_last, p):
    batch, seq, d_model = x.shape
    t = min(SEQ_TILE, seq)
    assert seq % t == 0 and t % ML_CHUNK == 0 and d_model % (ML_HEADS * LANES) == 0

    def per_layer(arr):
        shape = arr.shape[1:]
        return pl.BlockSpec((None,) + shape, lambda b, j: (layer,) + (0,) * len(shape),
                            pipeline_mode=pl.Buffered(1))

    weights = [p["norm_g"], p["w_in"], p["rg_conv_w"], p["rg_conv_b"], p["rg_w_a"], p["rg_b_a"],
               p["rg_w_x"], p["rg_b_x"], p["rg_lambda"], p["ml_conv_w"], p["ml_conv_b"],
               p["ml_w_q"], p["ml_w_k"], p["ml_w_v"], p["ml_w_if"], p["ml_b_if"], p["ml_norm_g"],
               p["w_out"]]
    in_specs = ([pl.BlockSpec((1, t, d_model), lambda b, j: (b, j, 0)),
                 pl.BlockSpec((1, 3, d_model), lambda b, j: (b, 0, 0))]
                + [per_layer(w) for w in weights]
                + [pl.BlockSpec(p["final_g"].shape, lambda b, j: (0, 0),
                                pipeline_mode=pl.Buffered(1))])
    dh = d_model // ML_HEADS
    return pl.pallas_call(
        functools.partial(_layer_kernel, is_last=is_last),
        grid=(batch, seq // t),
        in_specs=in_specs,
        out_specs=pl.BlockSpec((1, t, d_model), lambda b, j: (b, j, 0)),
        out_shape=jax.ShapeDtypeStruct(x.shape, F32),
        scratch_shapes=[
            pltpu.VMEM((t + SUBLANES, d_model), F32),
            pltpu.VMEM((t + SUBLANES, d_model), F32),
            pltpu.VMEM((1, d_model), F32),
            pltpu.VMEM((t, d_model), F32),
            pltpu.VMEM((ML_HEADS, dh, dh), F32),
            pltpu.VMEM((ML_HEADS, 1, dh), F32),
            pltpu.VMEM((SUBLANES, LANES), F32),
        ],
        compiler_params=pltpu.CompilerParams(
            dimension_semantics=("arbitrary", "arbitrary"),
            vmem_limit_bytes=VMEM_LIMIT_BYTES),
        name=f"hybrid_layer_{layer}",
    )(x, mod_l, *weights, p["final_g"])


def kernel(x, c, norm_g, w_ada, b_ada, w_in, rg_conv_w, rg_conv_b, rg_w_a, rg_b_a, rg_w_x, rg_b_x, rg_lambda, ml_conv_w, ml_conv_b, ml_w_q, ml_w_k, ml_w_v, ml_w_if, ml_b_if, ml_norm_g, w_out, final_g):
    depth, d_model = norm_g.shape
    dh = d_model // ML_HEADS
    assert ML_HEADS * 2 == SUBLANES

    def row(v):
        return v.reshape(depth, 1, v.shape[-1])

    k_scale = dh ** -0.5
    wif_t = jnp.swapaxes(ml_w_if, 1, 2)
    wif_t = wif_t * jnp.concatenate(
        [jnp.ones((d_model,), F32), jnp.full((d_model,), 1.0 / k_scale, F32),
         jnp.ones((d_model,), F32)])
    gate_rows = jnp.concatenate([wif_t[:, :ML_HEADS], wif_t[:, :ML_HEADS],
                                 wif_t[:, ML_HEADS:], wif_t[:, ML_HEADS:]], axis=1)
    bif = jnp.concatenate([ml_b_if[:, :ML_HEADS], ml_b_if[:, :ML_HEADS],
                           ml_b_if[:, ML_HEADS:], ml_b_if[:, ML_HEADS:]], axis=1)

    p = {
        "norm_g": row(norm_g), "w_in": w_in.astype(BF16),
        "rg_conv_w": rg_conv_w, "rg_conv_b": row(rg_conv_b),
        "rg_w_a": rg_w_a.astype(BF16), "rg_b_a": row(rg_b_a),
        "rg_w_x": rg_w_x.astype(BF16), "rg_b_x": row(rg_b_x), "rg_lambda": row(rg_lambda),
        "ml_conv_w": ml_conv_w, "ml_conv_b": row(ml_conv_b),
        "ml_w_q": ml_w_q.astype(BF16), "ml_w_k": (ml_w_k * k_scale).astype(BF16),
        "ml_w_v": ml_w_v.astype(BF16),
        "ml_w_if": gate_rows.astype(BF16), "ml_b_if": bif.reshape(depth, 2 * SUBLANES, 1),
        "ml_norm_g": row(ml_norm_g), "w_out": w_out.astype(BF16),
        "final_g": final_g.reshape(1, d_model),
    }
    mod = _modulation(c, w_ada, b_ada)
    for layer in range(depth):
        x = _layer(x, mod[layer], layer, layer == depth - 1, p)
    return x
```

```python
import functools

import jax
import jax.numpy as jnp
from jax import lax
from jax.experimental import pallas as pl
from jax.experimental.pallas import tpu as pltpu

F32 = jnp.float32
BF16 = jnp.bfloat16

EPS = 1e-6
RG_C = 8.0
CONV_WIDTH = 4
RG_HEADS = 8
ML_HEADS = 4
ML_CHUNK = 128
SUBLANES = 8
LANES = 128
MXU_WIDTH = 256
SEQ_TILE = 512
VMEM_LIMIT_BYTES = 60 * 1024 * 1024


def _dot(a, b):
    return jnp.dot(a, b, preferred_element_type=F32)


def _dot_nt(a, b):
    return lax.dot_general(a, b, (((1,), (1,)), ((), ())), preferred_element_type=F32)


def _dot_tn(a, b):
    return lax.dot_general(a, b, (((0,), (0,)), ((), ())), preferred_element_type=F32)


def _softplus(z):
    return jnp.maximum(z, 0.0) + jnp.log1p(jnp.exp(-jnp.abs(z)))


def _silu(z):
    return z * jax.nn.sigmoid(z)


def _rms_norm(x, g):
    return x * lax.rsqrt(jnp.mean(x * x, axis=-1, keepdims=True) + EPS) * g


def _interleave(*streams):
    streams = list(streams)
    while streams:
        for s in list(streams):
            try:
                next(s)
            except StopIteration:
                streams.remove(s)


def _causal_conv(u, ext_ref, w_ref, b_ref, cols):
    t = u.shape[0]
    ext_ref[SUBLANES:SUBLANES + t, cols] = u
    w = w_ref[:, cols]
    y = w[CONV_WIDTH - 1:CONV_WIDTH] * u + b_ref[:, cols]
    for k in range(1, CONV_WIDTH):
        y = y + (w[CONV_WIDTH - 1 - k:CONV_WIDTH - k]
                 * ext_ref[SUBLANES - k:SUBLANES - k + t, cols])
    ext_ref[0:SUBLANES, cols] = u[t - SUBLANES:t, :]
    return y


def _rg_scan(a, u, hcarry_ref, cols, out):
    t = a.shape[0]
    row = lax.broadcasted_iota(jnp.int32, a.shape, 0)
    d = 1
    while d < SUBLANES:
        a_sh = jnp.where(row >= d, pltpu.roll(a, d, 0), 1.0)
        u_sh = jnp.where(row >= d, pltpu.roll(u, d, 0), 0.0)
        u = a * u_sh + u
        a = a * a_sh
        d *= 2
        yield
    h = a[0:SUBLANES] * hcarry_ref[:, cols] + u[0:SUBLANES]
    hs = [h]
    for j in range(1, t // SUBLANES):
        rows = slice(j * SUBLANES, (j + 1) * SUBLANES)
        h = a[rows] * h + u[rows]
        hs.append(h)
    hcarry_ref[:, cols] = h[SUBLANES - 1:SUBLANES, :]
    out.append(jnp.concatenate(hs, axis=0))
    yield


def _mlstm_gates(li, lf):
    L = li.shape[1]
    r_idx = lax.broadcasted_iota(jnp.int32, (L, L), 0)
    c_idx = lax.broadcasted_iota(jnp.int32, (L, L), 1)
    b = jnp.dot(lf, (r_idx <= c_idx).astype(F32), precision=lax.Precision.HIGHEST,
                preferred_element_type=F32)
    g = li - b
    cols = jnp.transpose(
        jnp.concatenate([b, g, jnp.zeros((L - 2 * SUBLANES, L), F32)], axis=0))
    return g, cols, b[:, L - 1:L], jnp.max(g, axis=1, keepdims=True)


def _mlstm_head_chunk(h, q, qb, kf, kb, vb, g, cols, m_prev, m_last, c_old, n_old):
    L = q.shape[0]
    causal = (lax.broadcasted_iota(jnp.int32, (L, L), 0)
              >= lax.broadcasted_iota(jnp.int32, (L, L), 1))
    b_col = cols[:, h:h + 1]
    g_col = cols[:, SUBLANES + h:SUBLANES + h + 1]
    g_causal = jnp.where(causal, g[h:h + 1, :], -jnp.inf)
    m_col = jnp.maximum(m_prev, jnp.max(g_causal, axis=1, keepdims=True))
    w_inter_col = jnp.exp(m_prev - m_col)
    dec = jnp.exp(m_prev - m_last)

    s = _dot_nt(qb, kb) * jnp.exp(g_causal - m_col)
    num = _dot(s.astype(BF16), vb) + w_inter_col * _dot(qb, c_old.astype(BF16))
    den = (jnp.sum(s, axis=1, keepdims=True)
           + w_inter_col * jnp.sum(q * n_old, axis=1, keepdims=True))
    cell = num / jnp.maximum(jnp.abs(den), jnp.exp(-(b_col + m_col)))
    kw = kf * jnp.exp(g_col - m_last)
    c_new = dec * c_old + _dot_tn(kw.astype(BF16), vb)
    n_new = dec * n_old + jnp.sum(kw, axis=0, keepdims=True)
    return cell, c_new, n_new


def _layer_kernel(x_ref, mod_ref, ng_ref, w_in_ref,
                  rcw_ref, rcb_ref, rwa_ref, rba_ref, rwx_ref, rbx_ref, lam_ref,
                  mcw_ref, mcb_ref, wq_ref, wk_ref, wv_ref, wif_ref, bif_ref, mng_ref,
                  w_out_ref, fg_ref,
                  o_ref,
                  rgx_ext, mlx_ext, hcarry, c_st, n_st, m_st, *, is_last):
    d_model = x_ref.shape[2]
    t = x_ref.shape[1]
    rg_dh = d_model // RG_HEADS
    ml_dh = d_model // ML_HEADS

    @pl.when(pl.program_id(1) == 0)
    def _():
        rgx_ext[0:SUBLANES, :] = jnp.zeros((SUBLANES, d_model), F32)
        mlx_ext[0:SUBLANES, :] = jnp.zeros((SUBLANES, d_model), F32)
        hcarry[...] = jnp.zeros_like(hcarry)
        c_st[...] = jnp.zeros_like(c_st)
        n_st[...] = jnp.zeros_like(n_st)
        m_st[...] = jnp.zeros_like(m_st)

    x = x_ref[0]
    shift = mod_ref[0, 0:1, :]
    scale = mod_ref[0, 1:2, :]
    gate = mod_ref[0, 2:3, :]
    hb = (_rms_norm(x, ng_ref[...]) * (1.0 + scale) + shift).astype(BF16)

    def proj(i, cols):
        return _dot(hb, w_in_ref[:, i * d_model + cols.start:i * d_model + cols.stop])

    y_parts = {}

    def rg_stream():
        ys = []
        for cb in range(d_model // MXU_WIDTH):
            cols = slice(cb * MXU_WIDTH, (cb + 1) * MXU_WIDTH)
            u = proj(0, cols)
            yield
            xc = _causal_conv(u, rgx_ext, rcw_ref, rcb_ref, cols)
            yield
            xcb = xc.astype(BF16)
            heads = range(cols.start // rg_dh, cols.stop // rg_dh)

            def gate_dots(w_ref):
                return jnp.concatenate(
                    [_dot(xcb[:, hh * rg_dh - cols.start:(hh + 1) * rg_dh - cols.start], w_ref[hh])
                     for hh in heads], axis=1)

            r = jax.nn.sigmoid(gate_dots(rwa_ref) + rba_ref[:, cols])
            i_gate = jax.nn.sigmoid(gate_dots(rwx_ref) + rbx_ref[:, cols])
            yield
            log_a = (-RG_C) * r * _softplus(-lam_ref[:, cols])
            a = jnp.exp(log_a)
            mult = jnp.sqrt(-jnp.tanh(log_a) * (a * a + 1.0))
            uu = mult * (i_gate * xc)
            yield
            h_out = []
            yield from _rg_scan(a, uu, hcarry, cols, h_out)
            z = proj(1, cols)
            yield
            ys.append((h_out[0] * _silu(z)).astype(BF16))
            yield
        y_parts["rg"] = _dot(jnp.concatenate(ys, axis=1), w_out_ref[0:d_model, :])
        yield

    def ml_stream():
        qkv = []
        gates = bif_ref[...]
        for h in range(ML_HEADS):
            cols = slice(h * ml_dh, (h + 1) * ml_dh)
            ml_x = proj(2, cols)
            yield
            xc2b = _silu(_causal_conv(ml_x, mlx_ext, mcw_ref, mcb_ref, cols)).astype(BF16)
            ml_xb = ml_x.astype(BF16)
            yield
            q = _dot(xc2b, wq_ref[h])
            k = _dot(xc2b, wk_ref[h])
            v = _dot(ml_xb, wv_ref[h])
            yield
            qb, kb, vb = q.astype(BF16), k.astype(BF16), v.astype(BF16)
            gates = gates + (_dot_nt(wif_ref[:, cols], qb)
                             + _dot_nt(wif_ref[:, d_model + cols.start:d_model + cols.stop], kb)
                             + _dot_nt(wif_ref[:, 2 * d_model + cols.start:2 * d_model + cols.stop], vb))
            qkv.append((q, qb, k, kb, vb))
            yield
        li = gates[0:SUBLANES, :]
        lf = -_softplus(-gates[SUBLANES:2 * SUBLANES, :])
        n_chunks = t // ML_CHUNK
        chunk_gates = []
        m_prev = m_st[:, 0:1]
        for c in range(n_chunks):
            rs = slice(c * ML_CHUNK, (c + 1) * ML_CHUNK)
            g, cols_t, b_last, g_max = _mlstm_gates(li[:, rs], lf[:, rs])
            m_last = jnp.maximum(m_prev, g_max)
            chunk_gates.append((g, cols_t, m_prev, m_last))
            m_prev = b_last + m_last
            yield
        m_st[...] = jnp.broadcast_to(m_prev, m_st.shape)
        cells = [[] for _ in range(ML_HEADS)]
        c_state = [c_st[h] for h in range(ML_HEADS)]
        n_state = [n_st[h] for h in range(ML_HEADS)]
        for c in range(n_chunks):
            rs = slice(c * ML_CHUNK, (c + 1) * ML_CHUNK)
            g, cols_t, m_prev, m_last = chunk_gates[c]
            for h in range(ML_HEADS):
                q, qb, k, kb, vb = qkv[h]
                cell, c_state[h], n_state[h] = _mlstm_head_chunk(
                    h, q[rs], qb[rs], k[rs], kb[rs], vb[rs], g, cols_t,
                    m_prev[h:h + 1, :], m_last[h:h + 1, :], c_state[h], n_state[h])
                cells[h].append(cell)
                yield
        for h in range(ML_HEADS):
            c_st[h] = c_state[h]
            n_st[h] = n_state[h]
        ys = []
        for h in range(ML_HEADS):
            cols = slice(h * ml_dh, (h + 1) * ml_dh)
            hm = jax.nn.sigmoid(proj(3, cols)) * jnp.concatenate(cells[h], axis=0)
            yield
            hm = hm * lax.rsqrt(jnp.mean(hm * hm, axis=-1, keepdims=True) + EPS)
            z = proj(4, cols)
            yield
            ys.append((hm * mng_ref[:, cols] * _silu(z)).astype(BF16))
            yield
        y_parts["ml"] = _dot(jnp.concatenate(ys, axis=1), w_out_ref[d_model:2 * d_model, :])
        yield

    _interleave(rg_stream(), ml_stream())
    x_new = x + gate * (y_parts["rg"] + y_parts["ml"])
    if is_last:
        x_new = _rms_norm(x_new, fg_ref[...])
    o_ref[0] = x_new


def _mod_kernel(c_ref, w_ref, b_ref, o_ref):
    c = c_ref[...]
    o_ref[0] = _dot(_silu(c).astype(BF16), w_ref[0].astype(BF16)) + b_ref[0]


def _modulation(c, w_ada, b_ada):
    depth, d_model, _ = w_ada.shape
    batch = c.shape[0]
    rows = -(-batch // SUBLANES) * SUBLANES
    c_pad = jnp.pad(c, ((0, rows - batch), (0, 0)))
    out = pl.pallas_call(
        _mod_kernel,
        grid=(depth, 3),
        in_specs=[
            pl.BlockSpec((rows, d_model), lambda l, n: (0, 0)),
            pl.BlockSpec((1, d_model, d_model), lambda l, n: (l, 0, n)),
            pl.BlockSpec((1, 1, d_model), lambda l, n: (l, 0, n)),
        ],
        out_specs=pl.BlockSpec((1, rows, d_model), lambda l, n: (l, 0, n)),
        out_shape=jax.ShapeDtypeStruct((depth, rows, 3 * d_model), F32),
        compiler_params=pltpu.CompilerParams(
            dimension_semantics=("arbitrary", "arbitrary"),
            vmem_limit_bytes=VMEM_LIMIT_BYTES),
        name="adaln_modulation",
    )(c_pad, w_ada, b_ada.reshape(depth, 1, 3 * d_model))
    return out[:, :batch, :].reshape(depth, batch, 3, d_model)


def _layer(x, mod_l, layer, is_last, p):
    batch, seq, d_model = x.shape
    t = min(SEQ_TILE, seq)
    assert seq % t == 0 and t % ML_CHUNK == 0
    assert d_model // ML_HEADS == MXU_WIDTH and MXU_WIDTH % (d_model // RG_HEADS) == 0

    def per_layer(arr):
        shape = arr.shape[1:]
        return pl.BlockSpec((None,) + shape, lambda b, j: (layer,) + (0,) * len(shape),
                            pipeline_mode=pl.Buffered(1))

    weights = [p["norm_g"], p["w_in"], p["rg_conv_w"], p["rg_conv_b"], p["rg_w_a"], p["rg_b_a"],
               p["rg_w_x"], p["rg_b_x"], p["rg_lambda"], p["ml_conv_w"], p["ml_conv_b"],
               p["ml_w_q"], p["ml_w_k"], p["ml_w_v"], p["ml_w_if"], p["ml_b_if"], p["ml_norm_g"],
               p["w_out"]]
    in_specs = ([pl.BlockSpec((1, t, d_model), lambda b, j: (b, j, 0)),
                 pl.BlockSpec((1, 3, d_model), lambda b, j: (b, 0, 0))]
                + [per_layer(w) for w in weights]
                + [pl.BlockSpec(p["final_g"].shape, lambda b, j: (0, 0),
                                pipeline_mode=pl.Buffered(1))])
    dh = d_model // ML_HEADS
    return pl.pallas_call(
        functools.partial(_layer_kernel, is_last=is_last),
        grid=(batch, seq // t),
        in_specs=in_specs,
        out_specs=pl.BlockSpec((1, t, d_model), lambda b, j: (b, j, 0)),
        out_shape=jax.ShapeDtypeStruct(x.shape, F32),
        scratch_shapes=[
            pltpu.VMEM((t + SUBLANES, d_model), F32),
            pltpu.VMEM((t + SUBLANES, d_model), F32),
            pltpu.VMEM((1, d_model), F32),
            pltpu.VMEM((ML_HEADS, dh, dh), F32),
            pltpu.VMEM((ML_HEADS, 1, dh), F32),
            pltpu.VMEM((SUBLANES, LANES), F32),
        ],
        compiler_params=pltpu.CompilerParams(
            dimension_semantics=("arbitrary", "arbitrary"),
            vmem_limit_bytes=VMEM_LIMIT_BYTES),
        name=f"hybrid_layer_{layer}",
    )(x, mod_l, *weights, p["final_g"])


def kernel(x, c, norm_g, w_ada, b_ada, w_in, rg_conv_w, rg_conv_b, rg_w_a, rg_b_a, rg_w_x, rg_b_x, rg_lambda, ml_conv_w, ml_conv_b, ml_w_q, ml_w_k, ml_w_v, ml_w_if, ml_b_if, ml_norm_g, w_out, final_g):
    depth, d_model = norm_g.shape
    dh = d_model // ML_HEADS
    assert ML_HEADS * 2 == SUBLANES

    def row(v):
        return v.reshape(depth, 1, v.shape[-1])

    k_scale = dh ** -0.5
    wif_t = jnp.swapaxes(ml_w_if, 1, 2)
    wif_t = wif_t * jnp.concatenate(
        [jnp.ones((d_model,), F32), jnp.full((d_model,), 1.0 / k_scale, F32),
         jnp.ones((d_model,), F32)])
    gate_rows = jnp.concatenate([wif_t[:, :ML_HEADS], wif_t[:, :ML_HEADS],
                                 wif_t[:, ML_HEADS:], wif_t[:, ML_HEADS:]], axis=1)
    bif = jnp.concatenate([ml_b_if[:, :ML_HEADS], ml_b_if[:, :ML_HEADS],
                           ml_b_if[:, ML_HEADS:], ml_b_if[:, ML_HEADS:]], axis=1)

    p = {
        "norm_g": row(norm_g), "w_in": w_in.astype(BF16),
        "rg_conv_w": rg_conv_w, "rg_conv_b": row(rg_conv_b),
        "rg_w_a": rg_w_a.astype(BF16), "rg_b_a": row(rg_b_a),
        "rg_w_x": rg_w_x.astype(BF16), "rg_b_x": row(rg_b_x), "rg_lambda": row(rg_lambda),
        "ml_conv_w": ml_conv_w, "ml_conv_b": row(ml_conv_b),
        "ml_w_q": ml_w_q.astype(BF16), "ml_w_k": (ml_w_k * k_scale).astype(BF16),
        "ml_w_v": ml_w_v.astype(BF16),
        "ml_w_if": gate_rows.astype(BF16), "ml_b_if": bif.reshape(depth, 2 * SUBLANES, 1),
        "ml_norm_g": row(ml_norm_g), "w_out": w_out.astype(BF16),
        "final_g": final_g.reshape(1, d_model),
    }
    mod = _modulation(c, w_ada, b_ada)
    for layer in range(depth):
        x = _layer(x, mod[layer], layer, layer == depth - 1, p)
    return x
```

```python
import functools

import jax
import jax.numpy as jnp
from jax import lax
from jax.experimental import pallas as pl
from jax.experimental.pallas import tpu as pltpu

F32 = jnp.float32
BF16 = jnp.bfloat16

EPS = 1e-6
RG_C = 8.0
CONV_WIDTH = 4
RG_HEADS = 8
ML_HEADS = 4
ML_CHUNK = 128
SUBLANES = 8
LANES = 128
MXU_WIDTH = 256
SEQ_TILE = 512
VMEM_LIMIT_BYTES = 60 * 1024 * 1024


def _dot(a, b):
    return jnp.dot(a, b, preferred_element_type=F32)


def _dot_nt(a, b):
    return lax.dot_general(a, b, (((1,), (1,)), ((), ())), preferred_element_type=F32)


def _dot_tn(a, b):
    return lax.dot_general(a, b, (((0,), (0,)), ((), ())), preferred_element_type=F32)


def _softplus(z):
    return jnp.maximum(z, 0.0) + jnp.log1p(jnp.exp(-jnp.abs(z)))


def _silu(z):
    return z * jax.nn.sigmoid(z)


def _rms_norm(x, g):
    return x * lax.rsqrt(jnp.mean(x * x, axis=-1, keepdims=True) + EPS) * g


def _causal_conv(u, ext_ref, w_ref, b_ref, cols):
    t = u.shape[0]
    ext_ref[SUBLANES:SUBLANES + t, cols] = u
    w = w_ref[:, cols]
    y = w[CONV_WIDTH - 1:CONV_WIDTH] * u + b_ref[:, cols]
    for k in range(1, CONV_WIDTH):
        y = y + (w[CONV_WIDTH - 1 - k:CONV_WIDTH - k]
                 * ext_ref[SUBLANES - k:SUBLANES - k + t, cols])
    ext_ref[0:SUBLANES, cols] = u[t - SUBLANES:t, :]
    return y


def _permute_rows(x, scr):
    t, d = x.shape
    seg = t // SUBLANES
    pitch = scr.shape[1] // SUBLANES
    for l in range(d // LANES):
        for s in range(SUBLANES):
            scr[l, s * pitch:s * pitch + seg, :] = x[s * seg:(s + 1) * seg, l * LANES:(l + 1) * LANES]
    return jnp.concatenate(
        [jnp.concatenate([scr[l, pl.ds(p, SUBLANES, stride=pitch), :] for p in range(seg)], axis=0)
         for l in range(d // LANES)], axis=1)


def _unpermute_rows(y, scr):
    t, d = y.shape
    seg = t // SUBLANES
    pitch = scr.shape[1] // SUBLANES
    for l in range(d // LANES):
        for p in range(seg):
            scr[l, pl.ds(p, SUBLANES, stride=pitch), :] = (
                y[p * SUBLANES:(p + 1) * SUBLANES, l * LANES:(l + 1) * LANES])
    return jnp.concatenate(
        [jnp.concatenate([scr[l, s * pitch:s * pitch + seg, :] for s in range(SUBLANES)], axis=0)
         for l in range(d // LANES)], axis=1)


def _sublane_roll1(x):
    return jnp.concatenate(
        [pltpu.roll(x[j:j + SUBLANES], 1, 0) for j in range(0, x.shape[0], SUBLANES)], axis=0)


def _causal_conv_perm(u, halo_ref, w_ref, b_ref, cols):
    t = u.shape[0]
    n_tail = (CONV_WIDTH - 1) * SUBLANES
    tail = u[t - n_tail:t, :]
    sub = lax.broadcasted_iota(jnp.int32, tail.shape, 0) % SUBLANES
    head = jnp.where(sub == 0, _sublane_roll1(halo_ref[:, cols]), _sublane_roll1(tail))
    halo_ref[:, cols] = tail
    ext = jnp.concatenate([head, u], axis=0)
    w = w_ref[:, cols]
    y = w[CONV_WIDTH - 1:CONV_WIDTH] * u + b_ref[:, cols]
    for k in range(1, CONV_WIDTH):
        off = n_tail - k * SUBLANES
        y = y + w[CONV_WIDTH - 1 - k:CONV_WIDTH - k] * ext[off:off + t, :]
    return y


def _rg_scan_perm(a, u, carry):
    t = a.shape[0]
    seg = t // SUBLANES
    h = u[0:SUBLANES]
    acc_a = a[0:SUBLANES]
    hs, accs = [h], [acc_a]
    for p in range(1, seg):
        rows = slice(p * SUBLANES, (p + 1) * SUBLANES)
        h = a[rows] * h + u[rows]
        acc_a = a[rows] * acc_a
        hs.append(h)
        accs.append(acc_a)
    sub = lax.broadcasted_iota(jnp.int32, h.shape, 0)
    fa, fu = acc_a, h
    d = 1
    while d < SUBLANES:
        fa_sh = jnp.where(sub >= d, pltpu.roll(fa, d, 0), 1.0)
        fu_sh = jnp.where(sub >= d, pltpu.roll(fu, d, 0), 0.0)
        fu = fa * fu_sh + fu
        fa = fa * fa_sh
        d *= 2
    ends = fa * carry + fu
    starts = jnp.where(sub == 0, carry, pltpu.roll(ends, 1, 0))
    h_all = jnp.concatenate([hp + ap * starts for hp, ap in zip(hs, accs)], axis=0)
    return h_all, ends[SUBLANES - 1:SUBLANES, :]


def _mlstm_gates(li, lf):
    L = li.shape[1]
    r_idx = lax.broadcasted_iota(jnp.int32, (L, L), 0)
    c_idx = lax.broadcasted_iota(jnp.int32, (L, L), 1)
    tri = (r_idx <= c_idx).astype(BF16)
    hi = lf.astype(BF16)
    rem = lf - hi.astype(F32)
    mid = rem.astype(BF16)
    lo = (rem - mid.astype(F32)).astype(BF16)
    b = _dot(hi, tri) + _dot(mid, tri) + _dot(lo, tri)
    g = li - b
    cols = jnp.transpose(
        jnp.concatenate([b, g, jnp.zeros((L - 2 * SUBLANES, L), F32)], axis=0))
    return g, cols, b[:, L - 1:L], jnp.max(g, axis=1, keepdims=True)


def _mlstm_weights(h, g, cols, m_prev, m_last):
    L = g.shape[1]
    causal = (lax.broadcasted_iota(jnp.int32, (L, L), 0)
              >= lax.broadcasted_iota(jnp.int32, (L, L), 1))
    b_col = cols[:, h:h + 1]
    g_col = cols[:, SUBLANES + h:SUBLANES + h + 1]
    g_causal = jnp.where(causal, g[h:h + 1, :], -jnp.inf)
    m_col = jnp.maximum(m_prev, jnp.max(g_causal, axis=1, keepdims=True))
    return dict(
        intra=jnp.exp(g_causal - m_col),
        inter=jnp.exp(m_prev - m_col),
        inv_stab=jnp.exp(-(b_col + m_col)),
        state=jnp.exp(g_col - m_last),
        decay=jnp.exp(m_prev - m_last))


def _layer_kernel(x_ref, mod_ref, ng_ref, w_in_ref,
                  rcw_ref, rcb_ref, rwa_ref, rba_ref, rwx_ref, rbx_ref, lam_ref,
                  mcw_ref, mcb_ref, wq_ref, wk_ref, wv_ref, wif_ref, bif_ref, mng_ref,
                  w_out_ref, fg_ref,
                  o_ref,
                  perm_in, perm_out, rgx_halo, mlx_ext, hcarry, c_st, n_st, m_st, *, is_last):
    d_model = x_ref.shape[2]
    t = x_ref.shape[1]
    rg_dh = d_model // RG_HEADS
    ml_dh = d_model // ML_HEADS
    n_cb = d_model // MXU_WIDTH
    n_chunks = t // ML_CHUNK

    @pl.when(pl.program_id(1) == 0)
    def _():
        rgx_halo[...] = jnp.zeros_like(rgx_halo)
        mlx_ext[0:SUBLANES, :] = jnp.zeros((SUBLANES, d_model), F32)
        hcarry[...] = jnp.zeros_like(hcarry)
        c_st[...] = jnp.zeros_like(c_st)
        n_st[...] = jnp.zeros_like(n_st)
        m_st[...] = jnp.zeros_like(m_st)

    x = x_ref[0]
    shift = mod_ref[0, 0:1, :]
    scale = mod_ref[0, 1:2, :]
    gate = mod_ref[0, 2:3, :]
    h = _rms_norm(x, ng_ref[...]) * (1.0 + scale) + shift
    hb = h.astype(BF16)
    hbp = _permute_rows(h, perm_in).astype(BF16)

    def cols_of(i):
        return slice(i * MXU_WIDTH, (i + 1) * MXU_WIDTH)

    def chunk_rows(c):
        return slice(c * ML_CHUNK, (c + 1) * ML_CHUNK)

    v = {}
    c_state = [c_st[i] for i in range(ML_HEADS)]
    n_state = [n_st[i] for i in range(ML_HEADS)]
    cells = [[] for _ in range(ML_HEADS)]
    y_acc = [None] * n_cb

    def rg_proj_x(cb):
        v["rg_u", cb] = _dot(hbp, w_in_ref[0 * n_cb + cb])

    def rg_conv(cb):
        xc = _causal_conv_perm(v.pop(("rg_u", cb)), rgx_halo, rcw_ref, rcb_ref, cols_of(cb))
        v["rg_xc", cb] = xc
        v["rg_xcb", cb] = xc.astype(BF16)

    def rg_gate_dots(cb):
        xcb = v.pop(("rg_xcb", cb))
        heads = range(cb * MXU_WIDTH // rg_dh, (cb + 1) * MXU_WIDTH // rg_dh)

        def gate_dots(w_ref):
            return jnp.concatenate(
                [_dot(xcb[:, hh * rg_dh - cb * MXU_WIDTH:(hh + 1) * rg_dh - cb * MXU_WIDTH], w_ref[hh])
                 for hh in heads], axis=1)

        v["rg_ga", cb] = gate_dots(rwa_ref)
        v["rg_gi", cb] = gate_dots(rwx_ref)

    def rg_gates(cb):
        cols = cols_of(cb)
        r = jax.nn.sigmoid(v.pop(("rg_ga", cb)) + rba_ref[:, cols])
        i_gate = jax.nn.sigmoid(v.pop(("rg_gi", cb)) + rbx_ref[:, cols])
        log_a = (-RG_C) * r * _softplus(-lam_ref[:, cols])
        a = jnp.exp(log_a)
        mult = jnp.sqrt(-jnp.tanh(log_a) * (a * a + 1.0))
        v["rg_a", cb] = a
        v["rg_uu", cb] = mult * (i_gate * v.pop(("rg_xc", cb)))

    def rg_scan(cb):
        cols = cols_of(cb)
        v["rg_h", cb], hcarry[:, cols] = _rg_scan_perm(
            v.pop(("rg_a", cb)), v.pop(("rg_uu", cb)), hcarry[:, cols])

    def rg_proj_z(cb):
        v["rg_z", cb] = _dot(hbp, w_in_ref[1 * n_cb + cb])

    def rg_out(cb):
        v["rg_y", cb] = (v.pop(("rg_h", cb)) * _silu(v.pop(("rg_z", cb)))).astype(BF16)

    def rg_out_proj(nb):
        if "rg_y_all" not in v:
            v["rg_y_all"] = jnp.concatenate([v.pop(("rg_y", cb)) for cb in range(n_cb)], axis=1)
        v["rg_yp", nb] = _dot(v["rg_y_all"], w_out_ref[nb, 0:d_model, :])

    def rg_unpermute():
        y_perm = jnp.concatenate([v.pop(("rg_yp", nb)) for nb in range(n_cb)], axis=1)
        v["rg_y_nat"] = _unpermute_rows(y_perm, perm_out)

    def ml_proj_x(i):
        v["ml_x", i] = _dot(hb, w_in_ref[2 * n_cb + i])

    def ml_conv(i):
        ml_x = v.pop(("ml_x", i))
        v["ml_xc2b", i] = _silu(_causal_conv(ml_x, mlx_ext, mcw_ref, mcb_ref, cols_of(i))).astype(BF16)
        v["ml_xb", i] = ml_x.astype(BF16)

    def ml_qkv(i):
        xc2b = v.pop(("ml_xc2b", i))
        v["q", i] = _dot(xc2b, wq_ref[i])
        v["k", i] = _dot(xc2b, wk_ref[i])
        v["v", i] = _dot(v.pop(("ml_xb", i)), wv_ref[i])

    def ml_gate_dots(i):
        cols = cols_of(i)
        qb = v["q", i].astype(BF16)
        kb = v["k", i].astype(BF16)
        vb = v.pop(("v", i)).astype(BF16)
        v["qb", i], v["kb", i], v["vb", i] = qb, kb, vb
        v["gates"] = (v.get("gates", bif_ref[...])
                      + _dot_nt(wif_ref[:, cols], qb)
                      + _dot_nt(wif_ref[:, d_model + cols.start:d_model + cols.stop], kb)
                      + _dot_nt(wif_ref[:, 2 * d_model + cols.start:2 * d_model + cols.stop], vb))

    def ml_gates():
        gates = v.pop("gates")
        li = gates[0:SUBLANES, :]
        lf = -_softplus(-gates[SUBLANES:2 * SUBLANES, :])
        m_prev = m_st[:, 0:1]
        for c in range(n_chunks):
            g, cols_t, b_last, g_max = _mlstm_gates(li[:, chunk_rows(c)], lf[:, chunk_rows(c)])
            m_last = jnp.maximum(m_prev, g_max)
            v["chunk_gates", c] = (g, cols_t, m_prev, m_last)
            m_prev = b_last + m_last
        m_st[...] = jnp.broadcast_to(m_prev, m_st.shape)

    def ml_weights(c):
        g, cols_t, m_prev, m_last = v.pop(("chunk_gates", c))
        for i in range(ML_HEADS):
            v["w", c, i] = _mlstm_weights(i, g, cols_t, m_prev[i:i + 1, :], m_last[i:i + 1, :])

    def ml_scores(c):
        rs = chunk_rows(c)
        for i in range(ML_HEADS):
            v["s_raw", c, i] = _dot_nt(v["qb", i][rs], v["kb", i][rs])

    def ml_state_dots(c):
        rs = chunk_rows(c)
        for i in range(ML_HEADS):
            kw = v["k", i][rs] * v["w", c, i]["state"]
            v["kw_sum", c, i] = jnp.sum(kw, axis=0, keepdims=True)
            v["kwv", c, i] = _dot_tn(kw.astype(BF16), v["vb", i][rs])

    def ml_inter_dots(c):
        rs = chunk_rows(c)
        for i in range(ML_HEADS):
            v["q_c", c, i] = _dot(v["qb", i][rs], c_state[i].astype(BF16))
            v["q_n", c, i] = jnp.sum(v["q", i][rs] * n_state[i], axis=1, keepdims=True)

    def ml_score_weights(c):
        for i in range(ML_HEADS):
            s = v.pop(("s_raw", c, i)) * v["w", c, i]["intra"]
            v["s_sum", c, i] = jnp.sum(s, axis=1, keepdims=True)
            v["s_b", c, i] = s.astype(BF16)

    def ml_intra_dots(c):
        rs = chunk_rows(c)
        for i in range(ML_HEADS):
            v["s_v", c, i] = _dot(v.pop(("s_b", c, i)), v["vb", i][rs])

    def ml_state_update(c):
        for i in range(ML_HEADS):
            dec = v["w", c, i]["decay"]
            c_state[i] = dec * c_state[i] + v.pop(("kwv", c, i))
            n_state[i] = dec * n_state[i] + v.pop(("kw_sum", c, i))
        if c == n_chunks - 1:
            for i in range(ML_HEADS):
                c_st[i] = c_state[i]
                n_st[i] = n_state[i]

    def ml_cell(c):
        for i in range(ML_HEADS):
            w = v.pop(("w", c, i))
            num = v.pop(("s_v", c, i)) + w["inter"] * v.pop(("q_c", c, i))
            den = v.pop(("s_sum", c, i)) + w["inter"] * v.pop(("q_n", c, i))
            cells[i].append(num / jnp.maximum(jnp.abs(den), w["inv_stab"]))

    def ml_proj_o(i):
        v["ml_o", i] = _dot(hb, w_in_ref[3 * n_cb + i])

    def ml_proj_z(i):
        v["ml_z", i] = _dot(hb, w_in_ref[4 * n_cb + i])

    def ml_out(i):
        hm = jax.nn.sigmoid(v.pop(("ml_o", i))) * jnp.concatenate(cells[i], axis=0)
        hm = hm * lax.rsqrt(jnp.mean(hm * hm, axis=-1, keepdims=True) + EPS)
        v["ml_y", i] = (hm * mng_ref[:, cols_of(i)] * _silu(v.pop(("ml_z", i)))).astype(BF16)

    def ml_out_proj(i):
        y = v.pop(("ml_y", i))
        for nb in range(n_cb):
            part = _dot(y, w_out_ref[nb, d_model + i * ml_dh:d_model + (i + 1) * ml_dh, :])
            y_acc[nb] = part if y_acc[nb] is None else y_acc[nb] + part

    for i in range(n_cb):
        ml_proj_x(i)
        rg_proj_x(i)
        if i >= 1:
            ml_qkv(i - 1)
            rg_gate_dots(i - 1)
        ml_conv(i)
        rg_conv(i)
        if i >= 1:
            ml_gate_dots(i - 1)
            rg_gates(i - 1)
            rg_scan(i - 1)
    last = n_cb - 1
    rg_proj_z(0)
    ml_qkv(last)
    rg_gate_dots(last)
    ml_gate_dots(last)
    rg_gates(last)
    rg_scan(last)
    rg_out(0)
    rg_proj_z(1)
    ml_gates()
    rg_proj_z(2)
    rg_out(1)
    ml_weights(0)
    ml_scores(0)
    rg_proj_z(3)
    ml_state_dots(0)
    rg_out(2)
    ml_proj_o(0)
    rg_out(3)
    ml_proj_z(0)
    fillers = ([functools.partial(ml_proj_o, i) for i in range(1, ML_HEADS)]
               + [functools.partial(ml_proj_z, i) for i in range(1, ML_HEADS)]
               + [functools.partial(rg_out_proj, nb) for nb in range(n_cb)])
    fillers = [fillers[j] for j in (0, 3, 6, 1, 4, 7, 2, 5, 8, 9)] if len(fillers) == 10 else fillers

    def fill():
        if fillers:
            fillers.pop(0)()

    for c in range(n_chunks):
        ml_inter_dots(c)
        ml_score_weights(c)
        fill()
        if c + 1 < n_chunks:
            ml_weights(c + 1)
            ml_scores(c + 1)
            ml_state_dots(c + 1)
        ml_intra_dots(c)
        ml_state_update(c)
        fill()
        ml_cell(c)
    while fillers:
        fill()
    rg_unpermute()
    for i in range(ML_HEADS):
        ml_out(i)
        ml_out_proj(i)
    x_new = x + gate * (jnp.concatenate(y_acc, axis=1) + v.pop("rg_y_nat"))
    if is_last:
        x_new = _rms_norm(x_new, fg_ref[...])
    o_ref[0] = x_new


def _mod_kernel(c_ref, w_ref, b_ref, o_ref):
    c = c_ref[...]
    o_ref[0] = _dot(_silu(c).astype(BF16), w_ref[0].astype(BF16)) + b_ref[0]


def _modulation(c, w_ada, b_ada):
    depth, d_model, _ = w_ada.shape
    batch = c.shape[0]
    rows = -(-batch // SUBLANES) * SUBLANES
    c_pad = jnp.pad(c, ((0, rows - batch), (0, 0)))
    out = pl.pallas_call(
        _mod_kernel,
        grid=(depth, 3),
        in_specs=[
            pl.BlockSpec((rows, d_model), lambda l, n: (0, 0)),
            pl.BlockSpec((1, d_model, d_model), lambda l, n: (l, 0, n)),
            pl.BlockSpec((1, 1, d_model), lambda l, n: (l, 0, n)),
        ],
        out_specs=pl.BlockSpec((1, rows, d_model), lambda l, n: (l, 0, n)),
        out_shape=jax.ShapeDtypeStruct((depth, rows, 3 * d_model), F32),
        compiler_params=pltpu.CompilerParams(
            dimension_semantics=("arbitrary", "arbitrary"),
            vmem_limit_bytes=VMEM_LIMIT_BYTES),
        name="adaln_modulation",
    )(c_pad, w_ada, b_ada.reshape(depth, 1, 3 * d_model))
    return out[:, :batch, :].reshape(depth, batch, 3, d_model)


def _layer(x, mod_l, layer, is_last, p):
    batch, seq, d_model = x.shape
    t = min(SEQ_TILE, seq)
    assert seq % t == 0 and t % ML_CHUNK == 0
    assert d_model // ML_HEADS == MXU_WIDTH and MXU_WIDTH % (d_model // RG_HEADS) == 0
    pitch = t // SUBLANES + 4
    assert (t // SUBLANES) % SUBLANES == 0

    def per_layer(arr):
        shape = arr.shape[1:]
        return pl.BlockSpec((None,) + shape, lambda b, j: (layer,) + (0,) * len(shape),
                            pipeline_mode=pl.Buffered(1))

    weights = [p["norm_g"], p["w_in"], p["rg_conv_w"], p["rg_conv_b"], p["rg_w_a"], p["rg_b_a"],
               p["rg_w_x"], p["rg_b_x"], p["rg_lambda"], p["ml_conv_w"], p["ml_conv_b"],
               p["ml_w_q"], p["ml_w_k"], p["ml_w_v"], p["ml_w_if"], p["ml_b_if"], p["ml_norm_g"],
               p["w_out"]]
    in_specs = ([pl.BlockSpec((1, t, d_model), lambda b, j: (b, j, 0)),
                 pl.BlockSpec((1, 3, d_model), lambda b, j: (b, 0, 0))]
                + [per_layer(w) for w in weights]
                + [pl.BlockSpec(p["final_g"].shape, lambda b, j: (0, 0),
                                pipeline_mode=pl.Buffered(1))])
    dh = d_model // ML_HEADS
    return pl.pallas_call(
        functools.partial(_layer_kernel, is_last=is_last),
        grid=(batch, seq // t),
        in_specs=in_specs,
        out_specs=pl.BlockSpec((1, t, d_model), lambda b, j: (b, j, 0)),
        out_shape=jax.ShapeDtypeStruct(x.shape, F32),
        scratch_shapes=[
            pltpu.VMEM((d_model // LANES, SUBLANES * pitch, LANES), F32),
            pltpu.VMEM((d_model // LANES, SUBLANES * pitch, LANES), F32),
            pltpu.VMEM(((CONV_WIDTH - 1) * SUBLANES, d_model), F32),
            pltpu.VMEM((t + SUBLANES, d_model), F32),
            pltpu.VMEM((1, d_model), F32),
            pltpu.VMEM((ML_HEADS, dh, dh), F32),
            pltpu.VMEM((ML_HEADS, 1, dh), F32),
            pltpu.VMEM((SUBLANES, LANES), F32),
        ],
        compiler_params=pltpu.CompilerParams(
            dimension_semantics=("arbitrary", "arbitrary"),
            vmem_limit_bytes=VMEM_LIMIT_BYTES),
        name=f"hybrid_layer_{layer}",
    )(x, mod_l, *weights, p["final_g"])


def kernel(x, c, norm_g, w_ada, b_ada, w_in, rg_conv_w, rg_conv_b, rg_w_a, rg_b_a, rg_w_x, rg_b_x, rg_lambda, ml_conv_w, ml_conv_b, ml_w_q, ml_w_k, ml_w_v, ml_w_if, ml_b_if, ml_norm_g, w_out, final_g):
    depth, d_model = norm_g.shape
    dh = d_model // ML_HEADS
    assert ML_HEADS * 2 == SUBLANES

    def row(v):
        return v.reshape(depth, 1, v.shape[-1])

    def col_blocks(w):
        k, n = w.shape[1:]
        return jnp.transpose(w.astype(BF16).reshape(depth, k, n // MXU_WIDTH, MXU_WIDTH), (0, 2, 1, 3))

    k_scale = dh ** -0.5
    wif_t = jnp.swapaxes(ml_w_if, 1, 2)
    wif_t = wif_t * jnp.concatenate(
        [jnp.ones((d_model,), F32), jnp.full((d_model,), 1.0 / k_scale, F32),
         jnp.ones((d_model,), F32)])
    gate_rows = jnp.concatenate([wif_t[:, :ML_HEADS], wif_t[:, :ML_HEADS],
                                 wif_t[:, ML_HEADS:], wif_t[:, ML_HEADS:]], axis=1)
    bif = jnp.concatenate([ml_b_if[:, :ML_HEADS], ml_b_if[:, :ML_HEADS],
                           ml_b_if[:, ML_HEADS:], ml_b_if[:, ML_HEADS:]], axis=1)

    p = {
        "norm_g": row(norm_g), "w_in": col_blocks(w_in),
        "rg_conv_w": rg_conv_w, "rg_conv_b": row(rg_conv_b),
        "rg_w_a": rg_w_a.astype(BF16), "rg_b_a": row(rg_b_a),
        "rg_w_x": rg_w_x.astype(BF16), "rg_b_x": row(rg_b_x), "rg_lambda": row(rg_lambda),
        "ml_conv_w": ml_conv_w, "ml_conv_b": row(ml_conv_b),
        "ml_w_q": ml_w_q.astype(BF16), "ml_w_k": (ml_w_k * k_scale).astype(BF16),
        "ml_w_v": ml_w_v.astype(BF16),
        "ml_w_if": gate_rows.astype(BF16), "ml_b_if": bif.reshape(depth, 2 * SUBLANES, 1),
        "ml_norm_g": row(ml_norm_g), "w_out": col_blocks(w_out),
        "final_g": final_g.reshape(1, d_model),
    }
    mod = _modulation(c, w_ada, b_ada)
    for layer in range(depth):
        x = _layer(x, mod[layer], layer, layer == depth - 1, p)
    return x
```

```python
import functools

import jax
import jax.numpy as jnp
from jax import lax
from jax.experimental import pallas as pl
from jax.experimental.pallas import tpu as pltpu

F32 = jnp.float32
BF16 = jnp.bfloat16

EPS = 1e-6
RG_C = 8.0
CONV_WIDTH = 4
RG_HEADS = 8
ML_HEADS = 4
ML_CHUNK = 128
SUBLANES = 8
LANES = 128
MXU_WIDTH = 256
SEQ_TILE = 512
VMEM_LIMIT_BYTES = 60 * 1024 * 1024


def _dot(a, b):
    return jnp.dot(a, b, preferred_element_type=F32)


def _dot_nt(a, b):
    return lax.dot_general(a, b, (((1,), (1,)), ((), ())), preferred_element_type=F32)


def _dot_tn(a, b):
    return lax.dot_general(a, b, (((0,), (0,)), ((), ())), preferred_element_type=F32)


def _softplus(z):
    return jnp.maximum(z, 0.0) + jnp.log1p(jnp.exp(-jnp.abs(z)))


def _silu(z):
    return z * jax.nn.sigmoid(z)


def _rms_norm(x, g):
    return x * lax.rsqrt(jnp.mean(x * x, axis=-1, keepdims=True) + EPS) * g


def _causal_conv(u, ext_ref, w_ref, b_ref, cols):
    t = u.shape[0]
    ext_ref[SUBLANES:SUBLANES + t, cols] = u
    w = w_ref[:, cols]
    y = w[CONV_WIDTH - 1:CONV_WIDTH] * u + b_ref[:, cols]
    for k in range(1, CONV_WIDTH):
        y = y + (w[CONV_WIDTH - 1 - k:CONV_WIDTH - k]
                 * ext_ref[SUBLANES - k:SUBLANES - k + t, cols])
    ext_ref[0:SUBLANES, cols] = u[t - SUBLANES:t, :]
    return y


def _permute_rows(x, scr):
    t, d = x.shape
    seg = t // SUBLANES
    pitch = scr.shape[1] // SUBLANES
    for l in range(d // LANES):
        for s in range(SUBLANES):
            scr[l, s * pitch:s * pitch + seg, :] = x[s * seg:(s + 1) * seg, l * LANES:(l + 1) * LANES]
    return jnp.concatenate(
        [jnp.concatenate([scr[l, pl.ds(p, SUBLANES, stride=pitch), :] for p in range(seg)], axis=0)
         for l in range(d // LANES)], axis=1)


def _unpermute_rows(y, scr):
    t, d = y.shape
    seg = t // SUBLANES
    pitch = scr.shape[1] // SUBLANES
    for l in range(d // LANES):
        for p in range(seg):
            scr[l, pl.ds(p, SUBLANES, stride=pitch), :] = (
                y[p * SUBLANES:(p + 1) * SUBLANES, l * LANES:(l + 1) * LANES])
    return jnp.concatenate(
        [jnp.concatenate([scr[l, s * pitch:s * pitch + seg, :] for s in range(SUBLANES)], axis=0)
         for l in range(d // LANES)], axis=1)


def _sublane_roll1(x):
    return jnp.concatenate(
        [pltpu.roll(x[j:j + SUBLANES], 1, 0) for j in range(0, x.shape[0], SUBLANES)], axis=0)


def _causal_conv_perm(u, halo_ref, w_ref, b_ref, cols):
    t = u.shape[0]
    n_tail = (CONV_WIDTH - 1) * SUBLANES
    tail = u[t - n_tail:t, :]
    sub = lax.broadcasted_iota(jnp.int32, tail.shape, 0) % SUBLANES
    head = jnp.where(sub == 0, _sublane_roll1(halo_ref[:, cols]), _sublane_roll1(tail))
    halo_ref[:, cols] = tail
    ext = jnp.concatenate([head, u], axis=0)
    w = w_ref[:, cols]
    y = w[CONV_WIDTH - 1:CONV_WIDTH] * u + b_ref[:, cols]
    for k in range(1, CONV_WIDTH):
        off = n_tail - k * SUBLANES
        y = y + w[CONV_WIDTH - 1 - k:CONV_WIDTH - k] * ext[off:off + t, :]
    return y


def _rg_scan_perm(a, u, carry):
    t = a.shape[0]
    seg = t // SUBLANES
    h = u[0:SUBLANES]
    acc_a = a[0:SUBLANES]
    hs, accs = [h], [acc_a]
    for p in range(1, seg):
        rows = slice(p * SUBLANES, (p + 1) * SUBLANES)
        h = a[rows] * h + u[rows]
        acc_a = a[rows] * acc_a
        hs.append(h)
        accs.append(acc_a)
    sub = lax.broadcasted_iota(jnp.int32, h.shape, 0)
    fa, fu = acc_a, h
    d = 1
    while d < SUBLANES:
        fa_sh = jnp.where(sub >= d, pltpu.roll(fa, d, 0), 1.0)
        fu_sh = jnp.where(sub >= d, pltpu.roll(fu, d, 0), 0.0)
        fu = fa * fu_sh + fu
        fa = fa * fa_sh
        d *= 2
    ends = fa * carry + fu
    starts = jnp.where(sub == 0, carry, pltpu.roll(ends, 1, 0))
    h_all = jnp.concatenate([hp + ap * starts for hp, ap in zip(hs, accs)], axis=0)
    return h_all, ends[SUBLANES - 1:SUBLANES, :]


def _mlstm_gates(li, lf):
    L = li.shape[1]
    r_idx = lax.broadcasted_iota(jnp.int32, (L, L), 0)
    c_idx = lax.broadcasted_iota(jnp.int32, (L, L), 1)
    tri = (r_idx <= c_idx).astype(BF16)
    hi = lf.astype(BF16)
    rem = lf - hi.astype(F32)
    mid = rem.astype(BF16)
    lo = (rem - mid.astype(F32)).astype(BF16)
    b = _dot(hi, tri) + _dot(mid, tri) + _dot(lo, tri)
    g = li - b
    cols = jnp.transpose(
        jnp.concatenate([b, g, jnp.zeros((L - 2 * SUBLANES, L), F32)], axis=0))
    return g, cols, b[:, L - 1:L], jnp.max(g, axis=1, keepdims=True)


def _mlstm_weights(h, g, cols, m_prev, m_last):
    L = g.shape[1]
    causal = (lax.broadcasted_iota(jnp.int32, (L, L), 0)
              >= lax.broadcasted_iota(jnp.int32, (L, L), 1))
    b_col = cols[:, h:h + 1]
    g_col = cols[:, SUBLANES + h:SUBLANES + h + 1]
    g_causal = jnp.where(causal, g[h:h + 1, :], -jnp.inf)
    m_col = jnp.maximum(m_prev, jnp.max(g_causal, axis=1, keepdims=True))
    return dict(
        intra=jnp.exp(g_causal - m_col),
        inter=jnp.exp(m_prev - m_col),
        inv_stab=jnp.exp(-(b_col + m_col)),
        state=jnp.exp(g_col - m_last),
        decay=jnp.exp(m_prev - m_last))


def _layer_kernel(x_ref, mod_ref, ng_ref, w_in_ref,
                  rcw_ref, rcb_ref, rwax_ref, rba_ref, rbx_ref, lam_ref,
                  mcw_ref, mcb_ref, wq_ref, wk_ref, wv_ref, wif_ref, bif_ref, mng_ref,
                  w_out_ref, fg_ref,
                  o_ref,
                  perm_in, perm_out, rgx_halo, mlx_ext, hcarry, c_st, n_st, m_st, *, layer, is_last):
    d_model = x_ref.shape[2]
    t = x_ref.shape[1]
    n_cb = d_model // MXU_WIDTH
    n_chunks = t // ML_CHUNK
    ng_ref, rcb_ref, rba_ref, rbx_ref, lam_ref, mcb_ref, mng_ref = (
        r.at[pl.ds(layer, 1)] for r in (ng_ref, rcb_ref, rba_ref, rbx_ref, lam_ref, mcb_ref, mng_ref))
    rcw_ref, mcw_ref = rcw_ref.at[layer], mcw_ref.at[layer]

    @pl.when(pl.program_id(1) == 0)
    def _():
        rgx_halo[...] = jnp.zeros_like(rgx_halo)
        mlx_ext[0:SUBLANES, :] = jnp.zeros((SUBLANES, d_model), F32)
        hcarry[...] = jnp.zeros_like(hcarry)
        c_st[...] = jnp.zeros_like(c_st)
        n_st[...] = jnp.zeros_like(n_st)
        m_st[...] = jnp.zeros_like(m_st)

    x = x_ref[0]
    mod = mod_ref[pl.ds(pl.program_id(0), 1), :]
    shift = mod[:, 0:d_model]
    scale = mod[:, d_model:2 * d_model]
    gate = mod[:, 2 * d_model:3 * d_model]
    h = _rms_norm(x, ng_ref[...]) * (1.0 + scale) + shift
    hb = h.astype(BF16)
    hbp = _permute_rows(h, perm_in).astype(BF16)

    def cols_of(i):
        return slice(i * MXU_WIDTH, (i + 1) * MXU_WIDTH)

    def chunk_rows(c):
        return slice(c * ML_CHUNK, (c + 1) * ML_CHUNK)

    v = {}
    c_state = [c_st[i] for i in range(ML_HEADS)]
    n_state = [n_st[i] for i in range(ML_HEADS)]
    cells = [[] for _ in range(ML_HEADS)]

    def rg_proj_x(cb):
        v["rg_u", cb] = _dot(hbp, w_in_ref[0 * n_cb + cb])

    def rg_conv(cb):
        xc = _causal_conv_perm(v.pop(("rg_u", cb)), rgx_halo, rcw_ref, rcb_ref, cols_of(cb))
        v["rg_xc", cb] = xc
        v["rg_xcb", cb] = xc.astype(BF16)

    def rg_gate_dots(cb):
        both = _dot(v.pop(("rg_xcb", cb)), rwax_ref[cb])
        v["rg_ga", cb] = both[:, 0:MXU_WIDTH]
        v["rg_gi", cb] = both[:, MXU_WIDTH:2 * MXU_WIDTH]

    def rg_gates(cb):
        cols = cols_of(cb)
        r = jax.nn.sigmoid(v.pop(("rg_ga", cb)) + rba_ref[:, cols])
        i_gate = jax.nn.sigmoid(v.pop(("rg_gi", cb)) + rbx_ref[:, cols])
        log_a = (-RG_C) * r * _softplus(-lam_ref[:, cols])
        a = jnp.exp(log_a)
        mult = jnp.sqrt(-jnp.tanh(log_a) * (a * a + 1.0))
        v["rg_a", cb] = a
        v["rg_uu", cb] = mult * (i_gate * v.pop(("rg_xc", cb)))

    def rg_scan(cb):
        cols = cols_of(cb)
        v["rg_h", cb], hcarry[:, cols] = _rg_scan_perm(
            v.pop(("rg_a", cb)), v.pop(("rg_uu", cb)), hcarry[:, cols])

    def rg_proj_z(cb):
        v["rg_z", cb] = _dot(hbp, w_in_ref[1 * n_cb + cb])

    def rg_out(cb):
        v["rg_y", cb] = (v.pop(("rg_h", cb)) * _silu(v.pop(("rg_z", cb)))).astype(BF16)

    def rg_out_proj(nb):
        if "rg_y_all" not in v:
            v["rg_y_all"] = jnp.concatenate([v.pop(("rg_y", cb)) for cb in range(n_cb)], axis=1)
        v["rg_yp", nb] = _dot(v["rg_y_all"], w_out_ref[nb, 0:d_model, :])

    def rg_unpermute():
        y_perm = jnp.concatenate([v.pop(("rg_yp", nb)) for nb in range(n_cb)], axis=1)
        v["rg_y_nat"] = _unpermute_rows(y_perm, perm_out)

    def ml_proj_x(i):
        v["ml_x", i] = _dot(hb, w_in_ref[2 * n_cb + i])

    def ml_conv(i):
        ml_x = v.pop(("ml_x", i))
        v["ml_xc2b", i] = _silu(_causal_conv(ml_x, mlx_ext, mcw_ref, mcb_ref, cols_of(i))).astype(BF16)
        v["ml_xb", i] = ml_x.astype(BF16)

    def ml_qkv(i):
        xc2b = v.pop(("ml_xc2b", i))
        v["q", i] = _dot(xc2b, wq_ref[i])
        v["k", i] = _dot(xc2b, wk_ref[i])
        v["v", i] = _dot(v.pop(("ml_xb", i)), wv_ref[i])

    def ml_gate_dots(i):
        cols = cols_of(i)
        qb = v["q", i].astype(BF16)
        kb = v["k", i].astype(BF16)
        vb = v.pop(("v", i)).astype(BF16)
        v["qb", i], v["kb", i], v["vb", i] = qb, kb, vb
        v["gates"] = (v.get("gates", bif_ref[...])
                      + _dot_nt(wif_ref[:, cols], qb)
                      + _dot_nt(wif_ref[:, d_model + cols.start:d_model + cols.stop], kb)
                      + _dot_nt(wif_ref[:, 2 * d_model + cols.start:2 * d_model + cols.stop], vb))

    def ml_gates():
        gates = v.pop("gates")
        li = gates[0:SUBLANES, :]
        lf = -_softplus(-gates[SUBLANES:2 * SUBLANES, :])
        m_prev = m_st[:, 0:1]
        for c in range(n_chunks):
            g, cols_t, b_last, g_max = _mlstm_gates(li[:, chunk_rows(c)], lf[:, chunk_rows(c)])
            m_last = jnp.maximum(m_prev, g_max)
            v["chunk_gates", c] = (g, cols_t, m_prev, m_last)
            m_prev = b_last + m_last
        m_st[...] = jnp.broadcast_to(m_prev, m_st.shape)

    def ml_weights(c):
        g, cols_t, m_prev, m_last = v.pop(("chunk_gates", c))
        for i in range(ML_HEADS):
            v["w", c, i] = _mlstm_weights(i, g, cols_t, m_prev[i:i + 1, :], m_last[i:i + 1, :])

    def ml_scores(c):
        rs = chunk_rows(c)
        for i in range(ML_HEADS):
            v["s_raw", c, i] = _dot_nt(v["qb", i][rs], v["kb", i][rs])

    def ml_state_dots(c):
        rs = chunk_rows(c)
        for i in range(ML_HEADS):
            kw = v["k", i][rs] * v["w", c, i]["state"]
            v["kw_sum", c, i] = jnp.sum(kw, axis=0, keepdims=True)
            v["kwv", c, i] = _dot_tn(kw.astype(BF16), v["vb", i][rs])

    def ml_inter_dots(c):
        rs = chunk_rows(c)
        for i in range(ML_HEADS):
            v["q_c", c, i] = _dot(v["qb", i][rs], c_state[i].astype(BF16))
            v["q_n", c, i] = jnp.sum(v["q", i][rs] * n_state[i], axis=1, keepdims=True)

    def ml_score_weights(c):
        for i in range(ML_HEADS):
            s = v.pop(("s_raw", c, i)) * v["w", c, i]["intra"]
            v["s_sum", c, i] = jnp.sum(s, axis=1, keepdims=True)
            v["s_b", c, i] = s.astype(BF16)

    def ml_intra_dots(c):
        rs = chunk_rows(c)
        for i in range(ML_HEADS):
            v["s_v", c, i] = _dot(v.pop(("s_b", c, i)), v["vb", i][rs])

    def ml_state_update(c):
        for i in range(ML_HEADS):
            dec = v["w", c, i]["decay"]
            c_state[i] = dec * c_state[i] + v.pop(("kwv", c, i))
            n_state[i] = dec * n_state[i] + v.pop(("kw_sum", c, i))
        if c == n_chunks - 1:
            for i in range(ML_HEADS):
                c_st[i] = c_state[i]
                n_st[i] = n_state[i]

    def ml_cell(c):
        for i in range(ML_HEADS):
            w = v.pop(("w", c, i))
            num = v.pop(("s_v", c, i)) + w["inter"] * v.pop(("q_c", c, i))
            den = v.pop(("s_sum", c, i)) + w["inter"] * v.pop(("q_n", c, i))
            cells[i].append(num / jnp.maximum(jnp.abs(den), w["inv_stab"]))

    def ml_proj_o(i):
        v["ml_o", i] = _dot(hb, w_in_ref[3 * n_cb + i])

    def ml_proj_z(i):
        v["ml_z", i] = _dot(hb, w_in_ref[4 * n_cb + i])

    def ml_out(c):
        rs = chunk_rows(c)
        for i in range(ML_HEADS):
            hm = jax.nn.sigmoid(v["ml_o", i][rs]) * cells[i][c]
            hm = hm * lax.rsqrt(jnp.mean(hm * hm, axis=-1, keepdims=True) + EPS)
            v["ml_y", c, i] = (hm * mng_ref[:, cols_of(i)] * _silu(v["ml_z", i][rs])).astype(BF16)

    n_parts = 2 if n_chunks % 2 == 0 else 1
    part_chunks = n_chunks // n_parts
    part_rows = t // n_parts

    def ml_out_gather(r):
        v["ml_y_part", r] = jnp.concatenate(
            [jnp.concatenate([v.pop(("ml_y", c, i)) for i in range(ML_HEADS)], axis=1)
             for c in range(r * part_chunks, (r + 1) * part_chunks)], axis=0)

    def ml_out_proj(r, nb):
        v["ml_yo", r, nb] = _dot(v["ml_y_part", r], w_out_ref[nb, d_model:2 * d_model, :])

    def finish(r):
        rows = slice(r * part_rows, (r + 1) * part_rows)
        y = (jnp.concatenate([v.pop(("ml_yo", r, nb)) for nb in range(n_cb)], axis=1)
             + v["rg_y_nat"][rows])
        x_new = x[rows] + gate * y
        if is_last:
            x_new = _rms_norm(x_new, fg_ref[...])
        o_ref[0, rows, :] = x_new

    for i in range(n_cb):
        ml_proj_x(i)
        rg_proj_x(i)
        if i >= 1:
            ml_qkv(i - 1)
            rg_gate_dots(i - 1)
        ml_conv(i)
        rg_conv(i)
        if i >= 1:
            ml_gate_dots(i - 1)
            rg_gates(i - 1)
            rg_scan(i - 1)
    last = n_cb - 1
    rg_proj_z(0)
    ml_qkv(last)
    rg_gate_dots(last)
    ml_gate_dots(last)
    rg_gates(last)
    rg_scan(last)
    rg_out(0)
    rg_proj_z(1)
    ml_gates()
    rg_proj_z(2)
    rg_out(1)
    rg_proj_z(3)
    rg_out(2)

    def fill(queue, n=1):
        for _ in range(n):
            if queue:
                queue.pop(0)()

    big = [functools.partial(f, i) for i in range(ML_HEADS) for f in (ml_proj_o, ml_proj_z)]
    for c in range(n_chunks):
        ml_weights(c)
        fill(big)
        ml_scores(c)
        ml_state_dots(c)
        fill(big)
        ml_score_weights(c)
        if c >= 1:
            ml_intra_dots(c - 1)
    rg_out(3)
    fill(big, len(big))
    ml_intra_dots(n_chunks - 1)

    big = [functools.partial(rg_out_proj, nb) for nb in range(n_cb)] + [rg_unpermute]
    for c in range(n_chunks):
        ml_inter_dots(c)
        fill(big)
        ml_state_update(c)
        ml_cell(c)
        ml_out(c)
        fill(big)
        if (c + 1) % part_chunks == 0:
            r = c // part_chunks
            ml_out_gather(r)
            big += [functools.partial(ml_out_proj, r, nb) for nb in range(n_cb)]
            big += [functools.partial(finish, r)]
    fill(big, len(big))


def _mod_kernel(c_ref, w_ref, b_ref, o_ref):
    layer = pl.program_id(0)
    batch = c_ref.shape[0]
    rows = -(-batch // SUBLANES) * SUBLANES
    c = c_ref[...]
    c = jnp.concatenate([c, jnp.zeros((rows - batch, c.shape[1]), F32)], axis=0)
    y = _dot(_silu(c).astype(BF16), w_ref[...].astype(BF16))
    o_ref[...] = y[0:batch, :] + b_ref[pl.ds(layer, 1), :]


def _modulation(c, w_ada, b_ada):
    depth, d_model, d_mod = w_ada.shape
    batch = c.shape[0]
    return pl.pallas_call(
        _mod_kernel,
        grid=(depth,),
        in_specs=[
            pl.BlockSpec((batch, d_model), lambda l: (0, 0)),
            pl.BlockSpec((None, d_model, d_mod), lambda l: (l, 0, 0)),
            pl.BlockSpec((depth, d_mod), lambda l: (0, 0)),
        ],
        out_specs=pl.BlockSpec((None, batch, d_mod), lambda l: (l, 0, 0)),
        out_shape=jax.ShapeDtypeStruct((depth, batch, d_mod), F32),
        compiler_params=pltpu.CompilerParams(
            dimension_semantics=("arbitrary",),
            vmem_limit_bytes=VMEM_LIMIT_BYTES),
        name="adaln_modulation",
    )(c, w_ada, b_ada)


def _prep_kernel(w_in_ref, w_out_ref, wq_ref, wk_ref, wv_ref, wa_ref, wx_ref,
                 w_in_o, w_out_o, wq_o, wk_o, wv_o, wax_o, *, k_scale):
    for n in range(w_in_o.shape[0]):
        w_in_o[n] = w_in_ref[:, n * MXU_WIDTH:(n + 1) * MXU_WIDTH].astype(BF16)
    w_out_o[...] = w_out_ref[...].astype(BF16)

    @pl.when(pl.program_id(1) == 0)
    def _():
        wq_o[...] = wq_ref[...].astype(BF16)
        wk_o[...] = (wk_ref[...] * k_scale).astype(BF16)
        wv_o[...] = wv_ref[...].astype(BF16)
        rdh = wa_ref.shape[-1]
        per = MXU_WIDTH // rdh
        wax_o[...] = jnp.zeros_like(wax_o)
        for hh in range(wa_ref.shape[0]):
            cb, p = divmod(hh, per)
            rows = slice(p * rdh, (p + 1) * rdh)
            wax_o[cb, rows, p * rdh:(p + 1) * rdh] = wa_ref[hh].astype(BF16)
            wax_o[cb, rows, MXU_WIDTH + p * rdh:MXU_WIDTH + (p + 1) * rdh] = wx_ref[hh].astype(BF16)


def _prepare_weights(w_in, w_out, wq, wk, wv, wa, wx):
    depth, d_model, d_in = w_in.shape
    d_mix = w_out.shape[1]
    n_cb = d_model // MXU_WIDTH
    steps = n_cb
    in_blocks = d_in // MXU_WIDTH // steps
    dh, rdh = wq.shape[-1], wa.shape[-1]
    assert dh == MXU_WIDTH and MXU_WIDTH % rdh == 0 and d_in % (MXU_WIDTH * steps) == 0

    def whole(arr):
        shape = arr.shape[1:]
        return pl.BlockSpec((None,) + shape, lambda l, j: (l,) + (0,) * len(shape))

    out_shapes = [
        jax.ShapeDtypeStruct((depth, d_in // MXU_WIDTH, d_model, MXU_WIDTH), BF16),
        jax.ShapeDtypeStruct((depth, n_cb, d_mix, MXU_WIDTH), BF16),
        jax.ShapeDtypeStruct(wq.shape, BF16),
        jax.ShapeDtypeStruct(wk.shape, BF16),
        jax.ShapeDtypeStruct(wv.shape, BF16),
        jax.ShapeDtypeStruct((depth, n_cb, MXU_WIDTH, 2 * MXU_WIDTH), BF16),
    ]
    return pl.pallas_call(
        functools.partial(_prep_kernel, k_scale=dh ** -0.5),
        grid=(depth, steps),
        in_specs=[
            pl.BlockSpec((None, d_model, in_blocks * MXU_WIDTH), lambda l, j: (l, 0, j)),
            pl.BlockSpec((None, d_mix, MXU_WIDTH), lambda l, j: (l, 0, j)),
            whole(wq), whole(wk), whole(wv), whole(wa), whole(wx),
        ],
        out_specs=[
            pl.BlockSpec((None, in_blocks, d_model, MXU_WIDTH), lambda l, j: (l, j, 0, 0)),
            pl.BlockSpec((None, None, d_mix, MXU_WIDTH), lambda l, j: (l, j, 0, 0)),
            whole(out_shapes[2]), whole(out_shapes[3]), whole(out_shapes[4]), whole(out_shapes[5]),
        ],
        out_shape=out_shapes,
        compiler_params=pltpu.CompilerParams(
            dimension_semantics=("arbitrary", "arbitrary"),
            vmem_limit_bytes=VMEM_LIMIT_BYTES),
        name="prepare_weights",
    )(w_in, w_out, wq, wk, wv, wa, wx)


def _layer(x, mod, layer, is_last, p):
    batch, seq, d_model = x.shape
    t = min(SEQ_TILE, seq)
    assert seq % t == 0 and t % ML_CHUNK == 0
    assert d_model // ML_HEADS == MXU_WIDTH and MXU_WIDTH % (d_model // RG_HEADS) == 0
    pitch = t // SUBLANES + 4
    assert (t // SUBLANES) % SUBLANES == 0

    def per_layer(arr):
        shape = arr.shape[1:]
        return pl.BlockSpec((None,) + shape, lambda b, j: (layer,) + (0,) * len(shape),
                            pipeline_mode=pl.Buffered(1))

    def stacked(arr):
        return pl.BlockSpec(arr.shape, lambda b, j: (0,) * arr.ndim, pipeline_mode=pl.Buffered(1))

    operands = [
        (p["norm_g"], stacked), (p["w_in"], per_layer),
        (p["rg_conv_w"], stacked), (p["rg_conv_b"], stacked), (p["rg_w_ax"], per_layer),
        (p["rg_b_a"], stacked), (p["rg_b_x"], stacked), (p["rg_lambda"], stacked),
        (p["ml_conv_w"], stacked), (p["ml_conv_b"], stacked),
        (p["ml_w_q"], per_layer), (p["ml_w_k"], per_layer), (p["ml_w_v"], per_layer),
        (p["ml_w_if"], per_layer), (p["ml_b_if"], per_layer), (p["ml_norm_g"], stacked),
        (p["w_out"], per_layer), (p["final_g"], stacked)]
    in_specs = ([pl.BlockSpec((1, t, d_model), lambda b, j: (b, j, 0)), per_layer(mod)]
                + [spec(arr) for arr, spec in operands])
    weights = [arr for arr, _ in operands]
    dh = d_model // ML_HEADS
    return pl.pallas_call(
        functools.partial(_layer_kernel, layer=layer, is_last=is_last),
        grid=(batch, seq // t),
        in_specs=in_specs,
        out_specs=pl.BlockSpec((1, t, d_model), lambda b, j: (b, j, 0)),
        out_shape=jax.ShapeDtypeStruct(x.shape, F32),
        scratch_shapes=[
            pltpu.VMEM((d_model // LANES, SUBLANES * pitch, LANES), F32),
            pltpu.VMEM((d_model // LANES, SUBLANES * pitch, LANES), F32),
            pltpu.VMEM(((CONV_WIDTH - 1) * SUBLANES, d_model), F32),
            pltpu.VMEM((t + SUBLANES, d_model), F32),
            pltpu.VMEM((1, d_model), F32),
            pltpu.VMEM((ML_HEADS, dh, dh), F32),
            pltpu.VMEM((ML_HEADS, 1, dh), F32),
            pltpu.VMEM((SUBLANES, LANES), F32),
        ],
        compiler_params=pltpu.CompilerParams(
            dimension_semantics=("arbitrary", "arbitrary"),
            vmem_limit_bytes=VMEM_LIMIT_BYTES),
        name=f"hybrid_layer_{layer}",
    )(x, mod, *weights)


def kernel(x, c, norm_g, w_ada, b_ada, w_in, rg_conv_w, rg_conv_b, rg_w_a, rg_b_a, rg_w_x, rg_b_x, rg_lambda, ml_conv_w, ml_conv_b, ml_w_q, ml_w_k, ml_w_v, ml_w_if, ml_b_if, ml_norm_g, w_out, final_g):
    depth, d_model = norm_g.shape
    dh = d_model // ML_HEADS
    assert ML_HEADS * 2 == SUBLANES

    k_scale = dh ** -0.5
    wif_t = jnp.swapaxes(ml_w_if, 1, 2)
    wif_t = wif_t * jnp.concatenate(
        [jnp.ones((d_model,), F32), jnp.full((d_model,), 1.0 / k_scale, F32),
         jnp.ones((d_model,), F32)])
    gate_rows = jnp.concatenate([wif_t[:, :ML_HEADS], wif_t[:, :ML_HEADS],
                                 wif_t[:, ML_HEADS:], wif_t[:, ML_HEADS:]], axis=1)
    bif = jnp.concatenate([ml_b_if[:, :ML_HEADS], ml_b_if[:, :ML_HEADS],
                           ml_b_if[:, ML_HEADS:], ml_b_if[:, ML_HEADS:]], axis=1)

    w_in_b, w_out_b, wq_b, wk_b, wv_b, wax_b = _prepare_weights(
        w_in, w_out, ml_w_q, ml_w_k, ml_w_v, rg_w_a, rg_w_x)
    p = {
        "norm_g": norm_g, "w_in": w_in_b,
        "rg_conv_w": rg_conv_w, "rg_conv_b": rg_conv_b, "rg_w_ax": wax_b,
        "rg_b_a": rg_b_a, "rg_b_x": rg_b_x, "rg_lambda": rg_lambda,
        "ml_conv_w": ml_conv_w, "ml_conv_b": ml_conv_b,
        "ml_w_q": wq_b, "ml_w_k": wk_b, "ml_w_v": wv_b,
        "ml_w_if": gate_rows.astype(BF16), "ml_b_if": bif.reshape(depth, 2 * SUBLANES, 1),
        "ml_norm_g": ml_norm_g, "w_out": w_out_b,
        "final_g": final_g.reshape(1, d_model),
    }
    mod = _modulation(c, w_ada, b_ada)
    for layer in range(depth):
        x = _layer(x, mod, layer, layer == depth - 1, p)
    return x
```
